```python
import math
import jax
import jax.numpy as jnp
from jax import lax
import numpy as np

D_MODEL = 1024
BATCH = 8
SEQ = 2048
DEPTH = 2
DEC_BATCH = 128
DEC_SEQ = 1
PAST_LEN = 8192
PAGE_SIZE = 128

N_META = 16
ROPE_THETA = 500000.0
EPS = 1e-6
Q_BLOCK = 128

A_HEADS = 6
A_KV_HEADS = 2
A_HEAD_DIM = 64
A_ROT = A_HEAD_DIM // 4
IDX_HEADS = 8
IDX_DIM = 64
IDX_ROT = IDX_DIM // 4
TOPK_MAX = 256
B_HEADS = 6
B_NOPE = 64
B_ROPE = 32
B_V = 64
B_Q_RANK = 256
B_KV_RANK = 256
C_HEADS = 4
C_KV_HEADS = 2
C_QK = 32
C_V = 64
C_ROT = C_QK // 4

W_A = A_HEADS * A_HEAD_DIM
W_B = B_HEADS * B_V
W_C = C_HEADS * C_V
D_MIX = W_A + W_B + W_C

FF_RAW = -(-8 * D_MODEL // 3)
D_FF = -(-FF_RAW // 256) * 256

IN_SIZES = (A_HEADS * A_HEAD_DIM, A_KV_HEADS * A_HEAD_DIM, A_KV_HEADS * A_HEAD_DIM,
            IDX_HEADS * IDX_DIM, IDX_DIM, IDX_HEADS,
            B_Q_RANK, B_KV_RANK, B_ROPE,
            C_HEADS * 2 * C_QK, C_KV_HEADS * 2 * C_QK, C_KV_HEADS * C_V)
N_IN = sum(IN_SIZES)

kernel_name = 'hymba_dsa_mla_diff_step'


def rms_norm(x, g):
    xf = x.astype(jnp.float32)
    y = xf * lax.rsqrt(jnp.mean(xf * xf, axis=-1, keepdims=True) + EPS)
    return (y * g.astype(jnp.float32)).astype(x.dtype)


def rope(x, pos, rot):
    shp = x.shape
    xr = x.reshape(shp[0], shp[1], -1, shp[-1])
    inv = ROPE_THETA ** (-jnp.arange(0, rot, 2, dtype=jnp.float32) / rot)
    ang = pos.astype(jnp.float32)[..., None] * inv
    cos = jnp.cos(ang)[:, :, None, :]
    sin = jnp.sin(ang)[:, :, None, :]
    xf = xr[..., :rot].astype(jnp.float32)
    x1, x2 = xf[..., :rot // 2], xf[..., rot // 2:]
    rotated = jnp.concatenate([x1 * cos - x2 * sin, x2 * cos + x1 * sin], axis=-1).astype(x.dtype)
    return jnp.concatenate([rotated, xr[..., rot:]], axis=-1).reshape(shp)


def swiglu(h, g, wg, wu, wd):
    x = rms_norm(h, g)
    return (jax.nn.silu(x @ wg) * (x @ wu)) @ wd


_batched_take = jax.vmap(lambda a, i: a[i])


def mixer_inputs(h, pos, lp):
    B, T, _ = h.shape
    u = rms_norm(h, lp['g_attn']) @ lp['w_in']
    offs = np.cumsum(IN_SIZES)[:-1].tolist()
    (a_q, a_k, a_v, i_q, i_k, i_w, b_cq, b_ckv, b_kr, c_q, c_k, c_v) = jnp.split(u, offs, axis=-1)
    a_q = rope(rms_norm(a_q.reshape(B, T, A_HEADS, A_HEAD_DIM), lp['a_q_norm']), pos, A_ROT)
    a_k = rope(rms_norm(a_k.reshape(B, T, A_KV_HEADS, A_HEAD_DIM), lp['a_k_norm']), pos, A_ROT)
    a_v = a_v.reshape(B, T, A_KV_HEADS, A_HEAD_DIM)
    i_q = rope(i_q.reshape(B, T, IDX_HEADS, IDX_DIM), pos, IDX_ROT)
    i_k = rope(rms_norm(i_k, lp['idx_k_norm']), pos, IDX_ROT)
    i_w = i_w * (IDX_HEADS * IDX_DIM) ** -0.5
    b_q = (rms_norm(b_cq, lp['b_q_a_norm']) @ lp['w_uq']).reshape(B, T, B_HEADS, B_NOPE + B_ROPE)
    b_q = rms_norm(b_q, lp['b_q_norm'])
    b_qn = b_q[..., :B_NOPE]
    b_qr = rope(b_q[..., B_NOPE:], pos, B_ROPE)
    b_ckv = rms_norm(b_ckv, lp['b_kv_a_norm'])
    b_kr = rope(rms_norm(b_kr, lp['b_kr_norm']), pos, B_ROPE)
    c_q = rope(rms_norm(c_q.reshape(B, T, C_HEADS, 2, C_QK), lp['c_q_norm']), pos, C_ROT)
    c_k = rope(rms_norm(c_k.reshape(B, T, C_KV_HEADS, 2, C_QK), lp['c_k_norm']), pos, C_ROT)
    c_v = c_v.reshape(B, T, C_KV_HEADS, C_V)
    return (a_q, i_q, i_w, b_qn, b_qr, c_q), (a_k, a_v, i_k, b_ckv, b_kr, c_k, c_v)


def indexer_topk(i_q, i_w, i_k, q_pos, k_pos, topk):
    s = jax.nn.relu(jnp.einsum('bqhd,bld->bqhl', i_q, i_k).astype(jnp.float32))
    s = jnp.einsum('bqhl,bqh->bql', s, i_w.astype(jnp.float32))
    causal = k_pos[None, :] <= q_pos[:, None]
    s = jnp.where(causal[None], s, -jnp.inf)
    vals, idx = lax.top_k(s, topk)
    return idx, jnp.isfinite(vals)


def dsa_attend(q, k_sel, v_sel, valid):
    B, Q = q.shape[:2]
    qg = q.reshape(B, Q, A_KV_HEADS, A_HEADS // A_KV_HEADS, A_HEAD_DIM)
    s = jnp.einsum('bqgrd,bqkgd->bqgrk', qg, k_sel).astype(jnp.float32) * A_HEAD_DIM ** -0.5
    s = jnp.where(valid[:, :, None, None, :], s, -jnp.inf)
    p = jax.nn.softmax(s, axis=-1).astype(v_sel.dtype)
    return jnp.einsum('bqgrk,bqkgd->bqgrd', p, v_sel).reshape(B, Q, W_A)


def _stitch(out):
    out = jnp.moveaxis(out, 0, 1)
    return out.reshape(out.shape[0], -1, out.shape[-1])


def dsa_prompt(a_q, a_k, a_v, i_q, i_w, i_k, topk):
    L = a_q.shape[1]
    k_pos = jnp.arange(L)

    def block(start):
        q = lax.dynamic_slice_in_dim(a_q, start, Q_BLOCK, axis=1)
        iq = lax.dynamic_slice_in_dim(i_q, start, Q_BLOCK, axis=1)
        iw = lax.dynamic_slice_in_dim(i_w, start, Q_BLOCK, axis=1)
        idx, valid = indexer_topk(iq, iw, i_k, start + jnp.arange(Q_BLOCK), k_pos, topk)
        return dsa_attend(q, _batched_take(a_k, idx), _batched_take(a_v, idx), valid)

    return _stitch(lax.map(block, jnp.arange(0, L, Q_BLOCK)))


def mla_prompt(q_nope, q_rope, ckv, kr, w_uk, w_uv):
    B, L = q_nope.shape[:2]
    k_nope = jnp.einsum('blr,rhd->blhd', ckv, w_uk)
    v = jnp.einsum('blr,rhd->blhd', ckv, w_uv)
    k_pos = jnp.arange(L)
    scale = (B_NOPE + B_ROPE) ** -0.5

    def block(start):
        qn = lax.dynamic_slice_in_dim(q_nope, start, Q_BLOCK, axis=1)
        qr = lax.dynamic_slice_in_dim(q_rope, start, Q_BLOCK, axis=1)
        s = (jnp.einsum('bqhd,blhd->bhql', qn, k_nope)
             + jnp.einsum('bqhd,bld->bhql', qr, kr)).astype(jnp.float32) * scale
        mask = k_pos[None, :] <= (start + jnp.arange(Q_BLOCK))[:, None]
        p = jax.nn.softmax(jnp.where(mask, s, -jnp.inf), axis=-1).astype(v.dtype)
        return jnp.einsum('bhql,blhd->bqhd', p, v).reshape(B, Q_BLOCK, W_B)

    return _stitch(lax.map(block, jnp.arange(0, L, Q_BLOCK)))


def mla_sample(q_nope, q_rope, ckv_all, kr_all, w_uk, w_uv, mask):
    B, S = q_nope.shape[:2]
    q_abs = jnp.einsum('bshd,rhd->bshr', q_nope, w_uk)
    s = (jnp.einsum('bshr,blr->bhsl', q_abs, ckv_all)
         + jnp.einsum('bshd,bld->bhsl', q_rope, kr_all)).astype(jnp.float32) * (B_NOPE + B_ROPE) ** -0.5
    p = jax.nn.softmax(jnp.where(mask, s, -jnp.inf), axis=-1).astype(ckv_all.dtype)
    o_lat = jnp.einsum('bhsl,blr->bshr', p, ckv_all)
    return jnp.einsum('bshr,rhd->bshd', o_lat, w_uv).reshape(B, S, W_B)


def diff_attend(q, k, v, mask, lam, g_sub, lam_init):
    B, Q = q.shape[:2]
    qg = q.reshape(B, Q, C_KV_HEADS, C_HEADS // C_KV_HEADS, 2, C_QK)
    s = jnp.einsum('bqgrmd,blgmd->bgrmql', qg, k).astype(jnp.float32) * C_QK ** -0.5
    a = jax.nn.softmax(jnp.where(mask, s, -jnp.inf), axis=-1)
    a = a[:, :, :, 0] - lam * a[:, :, :, 1]
    o = jnp.einsum('bgrql,blgd->bqgrd', a.astype(v.dtype), v).reshape(B, Q, C_HEADS, C_V)
    o = rms_norm(o, g_sub) * (1.0 - lam_init)
    return o.reshape(B, Q, W_C)


def diff_prompt(q, k, v, lam, g_sub, lam_init):
    L = q.shape[1]
    k_pos = jnp.arange(L)

    def block(start):
        qb = lax.dynamic_slice_in_dim(q, start, Q_BLOCK, axis=1)
        mask = k_pos[None, :] <= (start + jnp.arange(Q_BLOCK))[:, None]
        return diff_attend(qb, k, v, mask, lam, g_sub, lam_init)

    return _stitch(lax.map(block, jnp.arange(0, L, Q_BLOCK)))


def gather_pages(cache, l, page_table):
    g = cache[l, page_table]
    return g.reshape(g.shape[0], -1, *g.shape[3:])


def gather_rows(cache, l, page_table, idx, new_rows, past_len):
    past = jnp.minimum(idx, past_len - 1)
    phys = jax.vmap(lambda pt, i: pt[i])(page_table, past // PAGE_SIZE)
    rows_past = cache[l, phys, past % PAGE_SIZE]
    new_i = jnp.clip(idx - past_len, 0, new_rows.shape[1] - 1)
    rows_new = _batched_take(new_rows, new_i)
    is_new = (idx >= past_len).reshape(idx.shape + (1,) * (rows_past.ndim - 3))
    return jnp.where(is_new, rows_new, rows_past)


def setup_inputs(seed: int = 0) -> dict:
    key = jax.random.key(seed)
    ks = iter(jax.random.split(key, 48))
    f32 = jnp.float32
    n_pages = PAST_LEN // PAGE_SIZE
    n_pool = (DEC_BATCH * n_pages * 5) // 4

    def nrm(shape, scale=1.0):
        return jax.random.normal(next(ks), shape, f32) * scale

    def gain(shape):
        return 1.0 + 0.02 * nrm(shape)

    pool = (DEPTH, n_pool, PAGE_SIZE)
    page_table = jax.random.permutation(next(ks), n_pool)[:DEC_BATCH * n_pages]
    page_table = page_table.reshape(DEC_BATCH, n_pages).astype(jnp.int32)
    return {
        'x_prompt': nrm((BATCH, SEQ, D_MODEL)),
        'x_sample': nrm((DEC_BATCH, DEC_SEQ, D_MODEL)),
        'cache_a_k': nrm(pool + (A_KV_HEADS, A_HEAD_DIM)),
        'cache_a_v': nrm(pool + (A_KV_HEADS, A_HEAD_DIM)),
        'cache_a_idx_k': nrm(pool + (IDX_DIM,)),
        'cache_b_ckv': nrm(pool + (B_KV_RANK,)),
        'cache_b_krope': nrm(pool + (B_ROPE,)),
        'cache_c_k': nrm(pool + (C_KV_HEADS, 2, C_QK)),
        'cache_c_v': nrm(pool + (C_KV_HEADS, C_V)),
        'page_table': page_table,
        'meta_tokens': nrm((N_META, D_MODEL)),
        'g_attn': gain((DEPTH, D_MODEL)),
        'w_in': nrm((DEPTH, D_MODEL, N_IN), D_MODEL ** -0.5),
        'a_q_norm': gain((DEPTH, A_HEAD_DIM)),
        'a_k_norm': gain((DEPTH, A_HEAD_DIM)),
        'idx_k_norm': gain((DEPTH, IDX_DIM)),
        'b_q_a_norm': gain((DEPTH, B_Q_RANK)),
        'w_uq': nrm((DEPTH, B_Q_RANK, B_HEADS * (B_NOPE + B_ROPE)), B_Q_RANK ** -0.5),
        'b_q_norm': gain((DEPTH, B_NOPE + B_ROPE)),
        'b_kv_a_norm': gain((DEPTH, B_KV_RANK)),
        'b_kr_norm': gain((DEPTH, B_ROPE)),
        'w_uk': nrm((DEPTH, B_KV_RANK, B_HEADS, B_NOPE), B_KV_RANK ** -0.5),
        'w_uv': nrm((DEPTH, B_KV_RANK, B_HEADS, B_V), B_KV_RANK ** -0.5),
        'c_q_norm': gain((DEPTH, C_QK)),
        'c_k_norm': gain((DEPTH, C_QK)),
        'c_lq1': nrm((DEPTH, C_QK), 0.1),
        'c_lk1': nrm((DEPTH, C_QK), 0.1),
        'c_lq2': nrm((DEPTH, C_QK), 0.1),
        'c_lk2': nrm((DEPTH, C_QK), 0.1),
        'c_sub_norm': gain((DEPTH, C_V)),
        'w_out': nrm((DEPTH, D_MIX, D_MODEL), D_MIX ** -0.5),
        'g_ffn': gain((DEPTH, D_MODEL)),
        'w_gate': nrm((DEPTH, D_MODEL, D_FF), D_MODEL ** -0.5),
        'w_up': nrm((DEPTH, D_MODEL, D_FF), D_MODEL ** -0.5),
        'w_down': nrm((DEPTH, D_FF, D_MODEL), D_FF ** -0.5),
    }


def reference(x_prompt, x_sample, cache_a_k, cache_a_v, cache_a_idx_k, cache_b_ckv, cache_b_krope,
              cache_c_k, cache_c_v, page_table, meta_tokens, g_attn, w_in, a_q_norm, a_k_norm,
              idx_k_norm, b_q_a_norm, w_uq, b_q_norm, b_kv_a_norm, b_kr_norm, w_uk, w_uv,
              c_q_norm, c_k_norm, c_lq1, c_lk1, c_lq2, c_lk2, c_sub_norm, w_out, g_ffn,
              w_gate, w_up, w_down):
    Bp, Sp, D = x_prompt.shape
    S = x_sample.shape[1]
    past_len = page_table.shape[1] * PAGE_SIZE
    l_p = Sp + N_META
    l_pad = -(-l_p // Q_BLOCK) * Q_BLOCK
    topk_p = min(TOPK_MAX, Sp // 4)
    topk_s = min(TOPK_MAX, (past_len + S) // 4)

    meta = jnp.broadcast_to(meta_tokens[None].astype(x_prompt.dtype), (Bp, N_META, D))
    h_p = jnp.concatenate([meta, x_prompt, jnp.zeros((Bp, l_pad - l_p, D), x_prompt.dtype)], axis=1)
    h_s = x_sample
    pos_p = jnp.arange(l_pad)[None]
    q_pos_s = past_len + jnp.arange(S)
    pos_s = q_pos_s[None]
    k_pos_s = jnp.arange(past_len + S)
    mask_s = k_pos_s[None, :] <= q_pos_s[:, None]

    rows_p = [[] for _ in range(7)]
    rows_s = [[] for _ in range(7)]
    for l in range(DEPTH):
        lp = {'g_attn': g_attn[l], 'w_in': w_in[l], 'a_q_norm': a_q_norm[l], 'a_k_norm': a_k_norm[l],
              'idx_k_norm': idx_k_norm[l], 'b_q_a_norm': b_q_a_norm[l], 'w_uq': w_uq[l],
              'b_q_norm': b_q_norm[l], 'b_kv_a_norm': b_kv_a_norm[l], 'b_kr_norm': b_kr_norm[l],
              'c_q_norm': c_q_norm[l], 'c_k_norm': c_k_norm[l]}
        lam_init = 0.8 - 0.6 * math.exp(-0.3 * l)
        lam = (jnp.exp(jnp.sum(c_lq1[l].astype(jnp.float32) * c_lk1[l].astype(jnp.float32)))
               - jnp.exp(jnp.sum(c_lq2[l].astype(jnp.float32) * c_lk2[l].astype(jnp.float32))) + lam_init)

        (a_q, i_q, i_w, b_qn, b_qr, c_q), rp = mixer_inputs(h_p, pos_p, lp)
        a_k, a_v, i_k, b_ckv, b_kr, c_k, c_v = rp
        o_a = dsa_prompt(a_q, a_k, a_v, i_q, i_w, i_k, topk_p)
        o_b = mla_prompt(b_qn, b_qr, b_ckv, b_kr, w_uk[l], w_uv[l])
        o_c = diff_prompt(c_q, c_k, c_v, lam, c_sub_norm[l], lam_init)
        h_p = h_p + jnp.concatenate([o_a, o_b, o_c], axis=-1) @ w_out[l]
        h_p = h_p + swiglu(h_p, g_ffn[l], w_gate[l], w_up[l], w_down[l])
        for lst, r in zip(rows_p, rp):
            lst.append(r[:, :l_p])

        (a_q, i_q, i_w, b_qn, b_qr, c_q), rs = mixer_inputs(h_s, pos_s, lp)
        a_k, a_v, i_k, b_ckv, b_kr, c_k, c_v = rs
        i_k_all = jnp.concatenate([gather_pages(cache_a_idx_k, l, page_table), i_k], axis=1)
        idx, valid = indexer_topk(i_q, i_w, i_k_all, q_pos_s, k_pos_s, topk_s)
        o_a = dsa_attend(a_q,
                         gather_rows(cache_a_k, l, page_table, idx, a_k, past_len),
                         gather_rows(cache_a_v, l, page_table, idx, a_v, past_len), valid)
        ckv_all = jnp.concatenate([gather_pages(cache_b_ckv, l, page_table), b_ckv], axis=1)
        kr_all = jnp.concatenate([gather_pages(cache_b_krope, l, page_table), b_kr], axis=1)
        o_b = mla_sample(b_qn, b_qr, ckv_all, kr_all, w_uk[l], w_uv[l], mask_s)
        ck_all = jnp.concatenate([gather_pages(cache_c_k, l, page_table), c_k], axis=1)
        cv_all = jnp.concatenate([gather_pages(cache_c_v, l, page_table), c_v], axis=1)
        o_c = diff_attend(c_q, ck_all, cv_all, mask_s, lam, c_sub_norm[l], lam_init)
        h_s = h_s + jnp.concatenate([o_a, o_b, o_c], axis=-1) @ w_out[l]
        h_s = h_s + swiglu(h_s, g_ffn[l], w_gate[l], w_up[l], w_down[l])
        for lst, r in zip(rows_s, rs):
            lst.append(r)

    y_prompt = h_p[:, N_META:l_p]
    y_sample = h_s
    (p_a_k, p_a_v, p_a_idx_k, p_b_ckv, p_b_krope, p_c_k, p_c_v) = [jnp.stack(r) for r in rows_p]
    (s_a_k, s_a_v, s_a_idx_k, s_b_ckv, s_b_krope, s_c_k, s_c_v) = [jnp.stack(r) for r in rows_s]
    return (y_prompt, y_sample, p_a_k, p_a_v, p_a_idx_k, p_b_ckv, p_b_krope, p_c_k, p_c_v,
            s_a_k, s_a_v, s_a_idx_k, s_b_ckv, s_b_krope, s_c_k, s_c_v)
```

```python
import functools
import math

import numpy as np
import jax
import jax.numpy as jnp
from jax import lax
from jax.experimental import pallas as pl
from jax.experimental.pallas import tpu as pltpu

F32 = jnp.float32
BF16 = jnp.bfloat16
I32 = jnp.int32
NEG_INF = float("-inf")

N_META = 16
ROPE_THETA = 500000.0
EPS = 1e-6
PAGE_SIZE = 128
A_HEADS, A_KV_HEADS, A_HEAD_DIM = 6, 2, 64
A_ROT = A_HEAD_DIM // 4
IDX_HEADS, IDX_DIM = 8, 64
IDX_ROT = IDX_DIM // 4
TOPK_MAX = 256
B_HEADS, B_NOPE, B_ROPE, B_V = 6, 64, 32, 64
B_Q_RANK, B_KV_RANK = 256, 256
C_HEADS, C_KV_HEADS, C_QK, C_V = 4, 2, 32, 64
C_ROT = C_QK // 4
IN_SIZES = (A_HEADS * A_HEAD_DIM, A_KV_HEADS * A_HEAD_DIM, A_KV_HEADS * A_HEAD_DIM,
            IDX_HEADS * IDX_DIM, IDX_DIM, IDX_HEADS,
            B_Q_RANK, B_KV_RANK, B_ROPE,
            C_HEADS * 2 * C_QK, C_KV_HEADS * 2 * C_QK, C_KV_HEADS * C_V)
(O_AQ, O_AK, O_AV, O_IQ, O_IK, O_IW, O_BCQ, O_BCKV, O_BKR, O_CQ, O_CK, O_CV) = (
    [0] + np.cumsum(IN_SIZES)[:-1].tolist())

LANE = 128
VMEM_LIMIT = 56 * 1024 * 1024

S_AQ = 0
S_AK = S_AQ + A_HEADS
S_AV = S_AK + 1
S_IQ = S_AV + 1
S_BCQ = S_IQ + IDX_HEADS
S_BCKV = S_BCQ + 2
S_CQ = S_BCKV + 2
S_CK = S_CQ + 2 * C_HEADS
S_CV = S_CK + 1
S_TAIL = S_CV + 1
N_SLAB = S_TAIL + 1
T_IK, T_KR, T_IW = 0, 64, 96

(TB_A_C, TB_A_M, TB_A_P, TB_C_C, TB_C_M, TB_C_P,
 TB_T_C, TB_T_M8, TB_T_P8, TB_T_M16, TB_T_P16) = range(11)
N_TAB = 11


def _in_proj_columns():
    src = -np.ones((N_SLAB, LANE), np.int64)
    for h in range(A_HEADS):
        g = h // (A_HEADS // A_KV_HEADS)
        src[S_AQ + h, g * 64:(g + 1) * 64] = O_AQ + h * 64 + np.arange(64)
    src[S_AK] = O_AK + np.arange(128)
    src[S_AV] = O_AV + np.arange(128)
    for h in range(IDX_HEADS):
        src[S_IQ + h, :64] = O_IQ + h * 64 + np.arange(64)
    src[S_BCQ:S_BCQ + 2] = (O_BCQ + np.arange(256)).reshape(2, 128)
    src[S_BCKV:S_BCKV + 2] = (O_BCKV + np.arange(256)).reshape(2, 128)
    for hd in range(C_HEADS):
        g = hd // (C_HEADS // C_KV_HEADS)
        for m in range(2):
            lo = g * 64 + m * 32
            src[S_CQ + hd * 2 + m, lo:lo + 32] = O_CQ + hd * 64 + m * 32 + np.arange(32)
    src[S_CK] = O_CK + np.arange(128)
    src[S_CV] = O_CV + np.arange(128)
    src[S_TAIL, T_IK:T_IK + 64] = O_IK + np.arange(64)
    src[S_TAIL, T_KR:T_KR + 32] = O_BKR + np.arange(32)
    src[S_TAIL, T_IW:T_IW + 8] = O_IW + np.arange(8)
    return src.reshape(-1)


def _uq_columns():
    src = -np.ones((2 * B_HEADS, LANE), np.int64)
    for h in range(B_HEADS):
        lo = (h % 2) * 64
        src[2 * h, lo:lo + 64] = h * (B_NOPE + B_ROPE) + np.arange(64)
        src[2 * h + 1, T_KR:T_KR + 32] = h * (B_NOPE + B_ROPE) + B_NOPE + np.arange(32)
    return src.reshape(-1)


def _take_cols(x, src, fill):
    src = np.asarray(src)
    parts, i = [], 0
    while i < len(src):
        j = i + 1
        if src[i] < 0:
            while j < len(src) and src[j] < 0:
                j += 1
            parts.append(jnp.full(x.shape[:-1] + (j - i,), fill, x.dtype))
        else:
            while j < len(src) and src[j] == src[j - 1] + 1:
                j += 1
            parts.append(x[..., int(src[i]):int(src[i]) + (j - i)])
        i = j
    return jnp.concatenate(parts, axis=-1)


def _gain_rows(a_q_norm, a_k_norm, idx_k_norm, b_kr_norm, b_q_norm, c_q_norm, c_k_norm):
    srcA = -np.ones((A_HEADS + 1, LANE), np.int64)
    for h in range(A_HEADS):
        g = h // (A_HEADS // A_KV_HEADS)
        srcA[h, g * 64:(g + 1) * 64] = np.arange(64)
    srcA[A_HEADS] = 64 + np.tile(np.arange(64), 2)
    gA = _take_cols(jnp.concatenate([a_q_norm, a_k_norm], -1), srcA.reshape(-1), 1.0)

    srcC = -np.ones((2 * C_HEADS + 1, LANE), np.int64)
    for hd in range(C_HEADS):
        g = hd // (C_HEADS // C_KV_HEADS)
        for m in range(2):
            lo = g * 64 + m * 32
            srcC[hd * 2 + m, lo:lo + 32] = np.arange(32)
    srcC[2 * C_HEADS] = 32 + np.tile(np.arange(32), 4)
    gC = _take_cols(jnp.concatenate([c_q_norm, c_k_norm], -1), srcC.reshape(-1), 1.0)

    srcT = -np.ones((LANE,), np.int64)
    srcT[T_IK:T_IK + 64] = np.arange(64)
    srcT[T_KR:T_KR + 32] = 64 + np.arange(32)
    gT = _take_cols(jnp.concatenate([idx_k_norm, b_kr_norm], -1), srcT, 1.0)

    srcB = -np.ones((2 * B_HEADS, LANE), np.int64)
    for h in range(B_HEADS):
        lo = (h % 2) * 64
        srcB[2 * h, lo:lo + 64] = np.arange(64)
        srcB[2 * h + 1, T_KR:T_KR + 32] = 64 + np.arange(32)
    gB = _take_cols(b_q_norm, srcB.reshape(-1), 1.0)
    return gA, gC, gT, gB


def _rope_tables(pos):
    posf = pos.astype(F32)[:, None]

    def cs(rot):
        inv = ROPE_THETA ** (-jnp.arange(0, rot, 2, dtype=F32) / rot)
        ang = posf * inv
        return jnp.cos(ang), jnp.sin(ang)

    L = pos.shape[0]
    one = lambda n: jnp.ones((L, n), F32)
    zero = lambda n: jnp.zeros((L, n), F32)
    cat = lambda xs: jnp.concatenate(xs, axis=-1)
    c8, s8 = cs(A_ROT)
    c4, s4 = cs(C_ROT)
    c16, s16 = cs(B_ROPE)
    tabs = [None] * N_TAB
    tabs[TB_A_C] = jnp.tile(cat([c8, c8, one(48)]), (1, 2))
    tabs[TB_A_M] = jnp.tile(cat([-s8, zero(56)]), (1, 2))
    tabs[TB_A_P] = jnp.tile(cat([zero(8), s8, zero(48)]), (1, 2))
    tabs[TB_C_C] = jnp.tile(cat([c4, c4, one(24)]), (1, 4))
    tabs[TB_C_M] = jnp.tile(cat([-s4, zero(28)]), (1, 4))
    tabs[TB_C_P] = jnp.tile(cat([zero(4), s4, zero(24)]), (1, 4))
    tabs[TB_T_C] = cat([c8, c8, one(48), c16, c16, one(32)])
    tabs[TB_T_M8] = cat([-s8, zero(120)])
    tabs[TB_T_P8] = cat([zero(8), s8, zero(112)])
    tabs[TB_T_M16] = cat([zero(64), -s16, zero(48)])
    tabs[TB_T_P16] = cat([zero(80), s16, zero(32)])
    return jnp.stack(tabs)


def _block_ones(block, scale):
    m = np.kron(np.eye(LANE // block), np.ones((block, block))) * scale
    return jnp.asarray(m, dtype=BF16)


def _tail_norm_matrix():
    m = np.zeros((LANE, LANE))
    m[T_IK:T_IK + 64, T_IK:T_IK + 64] = 1.0 / IDX_DIM
    m[T_KR:T_KR + 32, T_KR:T_KR + 32] = 1.0 / B_ROPE
    return jnp.asarray(m, dtype=BF16)


def _pick_tile(n, cap, mult):
    best = None
    for t in range(mult, min(n, cap) + 1, mult):
        if n % t == 0:
            best = t
    assert best is not None, (n, cap, mult)
    return best


def _dot(a, b):
    return jnp.dot(a, b, preferred_element_type=F32)


def _dot_nt(a, b):
    return lax.dot_general(a, b, (((1,), (1,)), ((), ())), preferred_element_type=F32)


def _full_rms(x, g):
    ms = jnp.mean(x * x, axis=-1, keepdims=True)
    return x * lax.rsqrt(ms + EPS) * g


def _slab(i, n=1):
    return slice(i * LANE, (i + n) * LANE)


def _proj_kernel(h_ref, gattn_ref, win_ref, tab_ref, gA_ref, gC_ref, gT_ref, gcq_ref, gckv_ref,
                 wuq_ref, gB_ref, wk_ref, wv_ref, m64_ref, m32_ref, mT_ref, ones_ref,
                 qa_ref, ka_ref, va_ref, qi_ref, ckv_ref, qc_ref, kc_ref, vc_ref, tail_ref, qb_ref,
                 *extra_refs, sample):
    x = h_ref[...]
    xn = _full_rms(x, gattn_ref[...]).astype(BF16)
    u = _dot(xn, win_ref[...])

    def rope(v, c, terms):
        out = v * tab_ref[c]
        for shift, t in terms:
            out = out + pltpu.roll(v, shift % LANE, axis=1) * tab_ref[t]
        return out

    rope_a = lambda v: rope(v, TB_A_C, ((-8, TB_A_M), (8, TB_A_P)))
    rope_c = lambda v: rope(v, TB_C_C, ((-4, TB_C_M), (4, TB_C_P)))
    rope_t = lambda v: rope(v, TB_T_C, ((-8, TB_T_M8), (8, TB_T_P8), (-16, TB_T_M16), (16, TB_T_P16)))

    def head_norm(v, m_ref, g):
        ms = _dot((v * v).astype(BF16), m_ref[...])
        return v * lax.rsqrt(ms + EPS) * g

    for i in range(A_HEADS + 1):
        v = head_norm(u[:, _slab(S_AQ + i)], m64_ref, gA_ref[:, _slab(i)])
        v = rope_a(v)
        if i < A_HEADS:
            qa_ref[:, _slab(i)] = v.astype(BF16)
        else:
            ka_ref[...] = v
    va_ref[...] = u[:, _slab(S_AV)]
    for h in range(IDX_HEADS):
        qi_ref[:, _slab(h)] = rope_a(u[:, _slab(S_IQ + h)]).astype(BF16)
    for i in range(2 * C_HEADS + 1):
        v = head_norm(u[:, _slab(S_CQ + i)], m32_ref, gC_ref[:, _slab(i)])
        v = rope_c(v)
        if i < 2 * C_HEADS:
            qc_ref[:, _slab(i)] = v.astype(BF16)
        else:
            kc_ref[...] = v
    vc_ref[...] = u[:, _slab(S_CV)]
    t = u[:, _slab(S_TAIL)]
    tn = rope_t(head_norm(t, mT_ref, gT_ref[...]))
    lane = lax.broadcasted_iota(I32, t.shape, 1)
    tail_ref[...] = jnp.where(lane < T_IW, tn, t * (IDX_HEADS * IDX_DIM) ** -0.5)
    ckv = _full_rms(u[:, _slab(S_BCKV, 2)], gckv_ref[...])
    ckv_ref[...] = ckv
    ckv_b = ckv.astype(BF16)
    cq = _full_rms(u[:, _slab(S_BCQ, 2)], gcq_ref[...]).astype(BF16)
    bq = _dot(cq, wuq_ref[...])
    for h in range(B_HEADS):
        n = bq[:, _slab(2 * h)]
        r = bq[:, _slab(2 * h + 1)]
        ms = (_dot((n * n).astype(BF16), ones_ref[...]) + _dot((r * r).astype(BF16), ones_ref[...])) * (
            1.0 / (B_NOPE + B_ROPE))
        inv = lax.rsqrt(ms + EPS)
        nn = (n * inv * gB_ref[:, _slab(2 * h)]).astype(BF16)
        qb_ref[:, _slab(2 * h)] = nn
        qb_ref[:, _slab(2 * h + 1)] = rope_t(r * inv * gB_ref[:, _slab(2 * h + 1)]).astype(BF16)
        if sample:
            extra_refs[0][:, h * B_KV_RANK:(h + 1) * B_KV_RANK] = _dot_nt(
                nn, wk_ref[:, _slab(h // 2)]).astype(BF16)
    if not sample:
        extra_refs[0][...] = _dot(ckv_b, wk_ref[...]).astype(BF16)
        extra_refs[1][...] = _dot(ckv_b, wv_ref[...]).astype(BF16)


def _proj_call(h3, tabs, lw, *, sample):
    B, L, D = h3.shape
    tm = _pick_tile(L, 288, 16)
    nl = L // tm
    row = lambda w: pl.BlockSpec((None, tm, w), lambda j, b: (b, j, 0))
    const = lambda a: pl.BlockSpec(a.shape, lambda j, b: (0,) * a.ndim)
    ins = [h3, lw["g_attn"], lw["w_in"], tabs, lw["gA"], lw["gC"], lw["gT"], lw["g_cq"], lw["g_ckv"],
           lw["w_uq"], lw["gB"], lw["w_uk"], lw["w_uv"], lw["m64"], lw["m32"], lw["mT"], lw["ones"]]
    in_specs = [row(D), const(lw["g_attn"]), const(lw["w_in"]),
                pl.BlockSpec((N_TAB, tm, LANE), lambda j, b: (0, j, 0))] + [const(a) for a in ins[4:]]
    names = ["qa", "ka", "va", "qi", "ckv", "qc", "kc", "vc", "tail", "qb"]
    widths = [A_HEADS * LANE, LANE, LANE, IDX_HEADS * LANE, B_KV_RANK, 2 * C_HEADS * LANE, LANE, LANE, LANE,
              2 * B_HEADS * LANE]
    dtypes = [BF16, F32, F32, BF16, F32, BF16, F32, F32, F32, BF16]
    if sample:
        names += ["qabs"]
        widths += [B_HEADS * B_KV_RANK]
        dtypes += [BF16]
    else:
        names += ["knope", "vb"]
        widths += [B_HEADS * B_NOPE, B_HEADS * B_V]
        dtypes += [BF16, BF16]
    outs = pl.pallas_call(
        functools.partial(_proj_kernel, sample=sample),
        grid=(nl, B),
        in_specs=in_specs,
        out_specs=[row(w) for w in widths],
        out_shape=[jax.ShapeDtypeStruct((B, L, w), dt) for w, dt in zip(widths, dtypes)],
        compiler_params=pltpu.CompilerParams(dimension_semantics=("arbitrary", "arbitrary"),
                                             vmem_limit_bytes=VMEM_LIMIT),
        name="proj_sample" if sample else "proj_prompt",
    )(*ins)
    return dict(zip(names, outs))


INT_MIN = -2 ** 31
NEG_INF_KEY = INT_MIN + 0x7FFFFF


def _topk_bias(s, k):
    R, L = s.shape
    s = jnp.where(s == 0.0, 0.0, s)
    bits = lax.bitcast_convert_type(s, I32)
    key = bits ^ ((bits >> 31) & 0x7FFFFFFF)
    kf = float(k)

    def count(mask):
        return jnp.sum(jnp.where(mask, 1.0, 0.0), axis=-1, keepdims=True)

    t0 = jnp.where(count(key >= 0) >= kf, 0, INT_MIN).astype(I32)

    def bit_step(i, t):
        cand = t + jnp.left_shift(jnp.int32(1), 30 - i)
        return jnp.where(count(key >= cand) >= kf, cand, t)

    thr = lax.fori_loop(0, 31, bit_step, t0)
    gt = key > thr
    eq = key == thr
    need = kf - count(gt)
    finite = key > NEG_INF_KEY
    col = lax.broadcasted_iota(I32, (R, L), 1)
    eqf = jnp.where(eq, 1.0, 0.0)
    nbits = max(1, (L - 1).bit_length())

    def col_step(i, p):
        cand = p + jnp.left_shift(jnp.int32(1), nbits - 1 - i)
        c = jnp.sum(jnp.where(col < cand, eqf, 0.0), axis=-1, keepdims=True)
        return jnp.where(c < need, cand, p)

    p = lax.fori_loop(0, nbits, col_step, jnp.zeros((R, 1), I32))
    return jnp.where((gt | (eq & (col <= p))) & finite, 0.0, NEG_INF)


def _softmax_unnorm(s):
    m = jnp.max(s, axis=-1, keepdims=True)
    e = jnp.exp(s - m)
    return e, jnp.sum(e, axis=-1, keepdims=True)


def _lambda(lq1, lk1, lq2, lk2, lam_init):
    return (jnp.exp(jnp.sum(lq1[...] * lk1[...], axis=-1, keepdims=True))
            - jnp.exp(jnp.sum(lq2[...] * lk2[...], axis=-1, keepdims=True)) + lam_init)


def _pattn_kernel(qa_ref, qi_ref, qt_ref, qb_ref, qc_ref,
                  ka_ref, va_ref, tl_ref, kn_ref, vb_ref, kc_ref, vc_ref,
                  lq1_ref, lk1_ref, lq2_ref, lk2_ref, gsub_ref,
                  o_ref, kab, vab, tlb, kcb, vcb, *, q0, topk, lam_init):
    tq = qa_ref.shape[0]
    lk = ka_ref.shape[0]
    qblk = pl.program_id(1)

    @pl.when(qblk == 0)
    def _():
        kab[...] = ka_ref[...].astype(BF16)
        vab[...] = va_ref[...].astype(BF16)
        tlb[...] = tl_ref[...].astype(BF16)
        kcb[...] = kc_ref[...].astype(BF16)
        vcb[...] = vc_ref[...].astype(BF16)

    rows = (q0 + qblk) * tq + lax.broadcasted_iota(I32, (tq, 1), 0)
    cols = lax.broadcasted_iota(I32, (1, lk), 1)
    causal = cols <= rows
    cbias = jnp.where(causal, 0.0, NEG_INF)

    tl = tlb[...]
    qt = qt_ref[...]
    idx = jnp.zeros((tq, lk), F32)
    for h in range(IDX_HEADS):
        sh = _dot_nt(qi_ref[:, _slab(h)], tl)
        idx = idx + qt[:, T_IW + h:T_IW + h + 1] * jnp.maximum(sh, 0.0)
    dbias = _topk_bias(jnp.where(causal, idx, NEG_INF), topk)

    pieces = []
    ka = kab[...]
    va = vab[...]
    for h in range(A_HEADS):
        g = h // (A_HEADS // A_KV_HEADS)
        s = _dot_nt(qa_ref[:, _slab(h)], ka) * A_HEAD_DIM ** -0.5 + dbias
        e, l = _softmax_unnorm(s)
        pv = _dot(e.astype(BF16), va) / l
        pieces.append(pv[:, g * 64:(g + 1) * 64])
    for h in range(B_HEADS):
        s = (_dot_nt(qb_ref[:, _slab(2 * h)], kn_ref[:, _slab(h // 2)])
             + _dot_nt(qb_ref[:, _slab(2 * h + 1)], tl)) * (B_NOPE + B_ROPE) ** -0.5 + cbias
        e, l = _softmax_unnorm(s)
        pv = _dot(e.astype(BF16), vb_ref[:, _slab(h // 2)]) / l
        pieces.append(pv[:, (h % 2) * 64:(h % 2 + 1) * 64])
    lam = _lambda(lq1_ref, lk1_ref, lq2_ref, lk2_ref, lam_init)
    kc = kcb[...]
    vc = vcb[...]
    for hd in range(C_HEADS):
        g = hd // (C_HEADS // C_KV_HEADS)
        e0, l0 = _softmax_unnorm(_dot_nt(qc_ref[:, _slab(2 * hd)], kc) * C_QK ** -0.5 + cbias)
        e1, l1 = _softmax_unnorm(_dot_nt(qc_ref[:, _slab(2 * hd + 1)], kc) * C_QK ** -0.5 + cbias)
        a = e0 * (1.0 / l0) - e1 * (lam / l1)
        pv = _dot(a.astype(BF16), vc)[:, g * 64:(g + 1) * 64]
        pieces.append(_full_rms(pv, gsub_ref[...]) * (1.0 - lam_init))
    o_ref[...] = jnp.concatenate(pieces, axis=-1).astype(o_ref.dtype)


def _pattn_call(pr, lw, *, q0, nqb, lk, tq, topk, lam_init):
    B = pr["qa"].shape[0]
    qrow = lambda a: pl.BlockSpec((None, tq, a.shape[-1]), lambda b, i: (b, q0 + i, 0))
    krow = lambda a: pl.BlockSpec((None, lk, a.shape[-1]), lambda b, i: (b, 0, 0))
    const = lambda a: pl.BlockSpec(a.shape, lambda b, i: (0,) * a.ndim)
    q_ins = [pr["qa"], pr["qi"], pr["tail"], pr["qb"], pr["qc"]]
    k_ins = [pr["ka"], pr["va"], pr["tail"], pr["knope"], pr["vb"], pr["kc"], pr["vc"]]
    c_ins = [lw["c_lq1"], lw["c_lk1"], lw["c_lq2"], lw["c_lk2"], lw["g_sub"]]
    d_mix = (A_HEADS + B_HEADS + C_HEADS) * 64
    return pl.pallas_call(
        functools.partial(_pattn_kernel, q0=q0, topk=topk, lam_init=lam_init),
        grid=(B, nqb),
        in_specs=[qrow(a) for a in q_ins] + [krow(a) for a in k_ins] + [const(a) for a in c_ins],
        out_specs=pl.BlockSpec((None, tq, d_mix), lambda b, i: (b, i, 0)),
        out_shape=jax.ShapeDtypeStruct((B, nqb * tq, d_mix), BF16),
        scratch_shapes=[pltpu.VMEM((lk, LANE), BF16) for _ in range(5)],
        compiler_params=pltpu.CompilerParams(dimension_semantics=("arbitrary", "arbitrary"),
                                             vmem_limit_bytes=VMEM_LIMIT),
        name=f"pattn_{q0}",
    )(*q_ins, *k_ins, *c_ins)


def _rows_t_kernel(*refs):
    n = len(refs) // 2
    for x_ref, o_ref in zip(refs[:n], refs[n:]):
        o_ref[...] = x_ref[...].T


def _rows_t_call(arrs):
    B, L, W = arrs[0].shape
    return pl.pallas_call(
        _rows_t_kernel,
        grid=(B,),
        in_specs=[pl.BlockSpec((None, L, W), lambda b: (b, 0, 0)) for _ in arrs],
        out_specs=[pl.BlockSpec((None, W, L), lambda b: (b, 0, 0)) for _ in arrs],
        out_shape=[jax.ShapeDtypeStruct((B, W, L), F32) for _ in arrs],
        compiler_params=pltpu.CompilerParams(dimension_semantics=("arbitrary",), vmem_limit_bytes=VMEM_LIMIT),
        name="rows_transpose",
    )(*arrs)


def _mlp_kernel(h_ref, o_ref, wout_ref, gffn_ref, wg_ref, wu_ref, wd_ref, y_ref, h1, xn, acc):
    f = pl.program_id(1)

    @pl.when(f == 0)
    def _():
        hh = h_ref[...] + _dot(o_ref[...], wout_ref[...])
        h1[...] = hh
        xn[...] = _full_rms(hh, gffn_ref[...]).astype(BF16)
        acc[...] = jnp.zeros_like(acc)

    x = xn[...]
    gate = _dot(x, wg_ref[...])
    up = _dot(x, wu_ref[...])
    act = gate * (1.0 / (1.0 + jnp.exp(-gate))) * up
    acc[...] += _dot(act.astype(BF16), wd_ref[...])

    @pl.when(f == pl.num_programs(1) - 1)
    def _():
        y_ref[...] = h1[...] + acc[...]


def _mlp_call(h2, o2, lw):
    M, D = h2.shape
    F = lw["w_gate"].shape[1]
    tm = _pick_tile(M, 512, 16)
    tf = _pick_tile(F, 768, LANE)
    return pl.pallas_call(
        _mlp_kernel,
        grid=(M // tm, F // tf),
        in_specs=[pl.BlockSpec((tm, D), lambda i, f: (i, 0)),
                  pl.BlockSpec((tm, o2.shape[1]), lambda i, f: (i, 0)),
                  pl.BlockSpec(lw["w_out"].shape, lambda i, f: (0, 0)),
                  pl.BlockSpec((1, D), lambda i, f: (0, 0)),
                  pl.BlockSpec((D, tf), lambda i, f: (0, f)),
                  pl.BlockSpec((D, tf), lambda i, f: (0, f)),
                  pl.BlockSpec((tf, D), lambda i, f: (f, 0))],
        out_specs=pl.BlockSpec((tm, D), lambda i, f: (i, 0)),
        out_shape=jax.ShapeDtypeStruct((M, D), F32),
        scratch_shapes=[pltpu.VMEM((tm, D), F32), pltpu.VMEM((tm, D), BF16), pltpu.VMEM((tm, D), F32)],
        compiler_params=pltpu.CompilerParams(dimension_semantics=("arbitrary", "arbitrary"),
                                             vmem_limit_bytes=VMEM_LIMIT),
        name="mlp",
    )(h2, o2, lw["w_out"], lw["g_ffn"], lw["w_gate"], lw["w_up"], lw["w_down"])


SUB_PAGES = 8


def _page_specs(cache, layer, group):
    blk = (None, None) + cache.shape[2:]
    return [pl.BlockSpec(blk, lambda b, c, pt, g=g: (layer, pt[b, c * group + g], 0, 0)) for g in range(group)]


def _pages_kt(page_refs):
    return jnp.concatenate([r[...] for r in page_refs], axis=1).astype(BF16)


def _sidx_kernel(pt_ref, q_ref, w_ref, *refs):
    page_refs, out_ref = refs[:-1], refs[-1]
    kt = _pages_kt(page_refs)
    s = jnp.maximum(_dot(q_ref[...], kt), 0.0) * w_ref[...]
    out_ref[...] = jnp.sum(s, axis=0, keepdims=True)


def _sidx_call(page_table, q, w, cache, layer):
    DB, n_pages = page_table.shape
    T = n_pages * PAGE_SIZE
    group = _pick_tile(n_pages, 16, 1)
    grid_spec = pltpu.PrefetchScalarGridSpec(
        num_scalar_prefetch=1, grid=(DB, n_pages // group),
        in_specs=[pl.BlockSpec((None, IDX_HEADS, IDX_DIM), lambda b, c, pt: (b, 0, 0)),
                  pl.BlockSpec((None, IDX_HEADS, 1), lambda b, c, pt: (b, 0, 0))]
        + _page_specs(cache, layer, group),
        out_specs=pl.BlockSpec((None, 1, group * PAGE_SIZE), lambda b, c, pt: (b, 0, c)))
    return pl.pallas_call(
        _sidx_kernel,
        grid_spec=grid_spec,
        out_shape=jax.ShapeDtypeStruct((DB, 1, T), F32),
        compiler_params=pltpu.CompilerParams(dimension_semantics=("arbitrary", "arbitrary"),
                                             vmem_limit_bytes=VMEM_LIMIT),
        name="sample_indexer",
    )(page_table, q, w, *([cache] * group))


def _sselect_kernel(sp_ref, qi_ref, tail_ref, bias_ref, bself_ref, *, topk):
    tl = tail_ref[...]
    tlb = tl.astype(BF16).astype(F32)
    own = jnp.zeros((tl.shape[0], 1), F32)
    for h in range(IDX_HEADS):
        d = jnp.sum(qi_ref[:, _slab(h)].astype(F32) * tlb, axis=-1, keepdims=True)
        own = own + tl[:, T_IW + h:T_IW + h + 1] * jnp.maximum(d, 0.0)
    lane = lax.broadcasted_iota(I32, tl.shape, 1)
    own_blk = jnp.where(lane == 0, own, NEG_INF)
    bias = _topk_bias(jnp.concatenate([sp_ref[...], own_blk], axis=-1), topk)
    T = sp_ref.shape[1]
    bias_ref[...] = bias[:, :T]
    bself_ref[...] = bias[:, T:]


def _sselect_call(s_past, qi, tail, topk):
    DB, T = s_past.shape
    full = lambda a: pl.BlockSpec(a.shape, lambda i: (0,) * a.ndim)
    return pl.pallas_call(
        functools.partial(_sselect_kernel, topk=topk),
        grid=(1,),
        in_specs=[full(s_past), full(qi), full(tail)],
        out_specs=[pl.BlockSpec((DB, T), lambda i: (0, 0)), pl.BlockSpec((DB, LANE), lambda i: (0, 0))],
        out_shape=[jax.ShapeDtypeStruct((DB, T), F32), jax.ShapeDtypeStruct((DB, LANE), F32)],
        compiler_params=pltpu.CompilerParams(dimension_semantics=("arbitrary",), vmem_limit_bytes=VMEM_LIMIT),
        name="sample_select",
    )(s_past, qi, tail)


N_SCACHE = 6


def _online_update(s, v, v_feature_major, m_ref, l_ref, acc_ref):
    m_old = m_ref[...]
    m_new = jnp.maximum(m_old, jnp.max(s, axis=-1, keepdims=True))
    m_safe = jnp.where(m_new == NEG_INF, 0.0, m_new)
    alpha = jnp.exp(m_old - m_safe)
    p = jnp.exp(s - m_safe)
    l_ref[...] = alpha * l_ref[...] + jnp.sum(p, axis=-1, keepdims=True)
    pb = p.astype(BF16)
    acc_ref[...] = alpha * acc_ref[...] + (_dot_nt(pb, v) if v_feature_major else _dot(pb, v))
    m_ref[...] = m_new


def _self_update(q, k_row, v_row, scale, bias, m_ref, l_ref, acc_ref):
    kb = k_row.astype(BF16).astype(F32)
    s = jnp.sum(q.astype(F32) * kb, axis=-1, keepdims=True) * scale + bias
    m_old = m_ref[...]
    m_new = jnp.maximum(m_old, s)
    m_safe = jnp.where(m_new == NEG_INF, 0.0, m_new)
    alpha = jnp.exp(m_old - m_safe)
    p = jnp.exp(s - m_safe)
    l_ref[...] = alpha * l_ref[...] + p
    acc_ref[...] = alpha * acc_ref[...] + p * v_row.astype(BF16).astype(F32)
    m_ref[...] = m_new


def _smain_kernel(pt_ref, qa_ref, qabs_ref, qr_ref, qc_ref, bias_ref, bself_ref,
                  nak_ref, nav_ref, nckv_ref, nkr_ref, nck_ref, ncv_ref, *refs, group):
    pak, pav, pckv, pkr, pck, pcv = [refs[i * group:(i + 1) * group] for i in range(N_SCACHE)]
    oa_ref, ol_ref, oc_ref, ma, la, acca, mb, lb, accb, mc, lc, accc = refs[N_SCACHE * group:]
    c = pl.program_id(1)
    n_chunks = pl.num_programs(1)

    @pl.when(c == 0)
    def _():
        for m_ref, l_ref, a_ref in ((ma, la, acca), (mb, lb, accb), (mc, lc, accc)):
            m_ref[...] = jnp.full_like(m_ref, NEG_INF)
            l_ref[...] = jnp.zeros_like(l_ref)
            a_ref[...] = jnp.zeros_like(a_ref)

    qa = qa_ref[...]
    qabs = qabs_ref[...]
    qr = qr_ref[...]
    qc = qc_ref[...]
    sc_a = A_HEAD_DIM ** -0.5
    sc_b = (B_NOPE + B_ROPE) ** -0.5
    sc_c = C_QK ** -0.5
    s = _dot(qa, _pages_kt(pak)) * sc_a + bias_ref[...]
    _online_update(s, _pages_kt(pav), True, ma, la, acca)
    lat = jnp.concatenate([r[...] for r in pckv], axis=0).astype(BF16)
    s = (_dot_nt(qabs, lat) + _dot(qr, _pages_kt(pkr))) * sc_b
    _online_update(s, lat, False, mb, lb, accb)
    s = _dot(qc, _pages_kt(pck)) * sc_c
    _online_update(s, _pages_kt(pcv), True, mc, lc, accc)

    @pl.when(c == n_chunks - 1)
    def _():
        _self_update(qa, nak_ref[...], nav_ref[...], sc_a, bself_ref[:, 0:1], ma, la, acca)
        kb = nckv_ref[...].astype(BF16).astype(F32)
        rb = nkr_ref[...].astype(BF16).astype(F32)
        s = (jnp.sum(qabs.astype(F32) * kb, axis=-1, keepdims=True)
             + jnp.sum(qr.astype(F32) * rb, axis=-1, keepdims=True)) * sc_b
        m_new = jnp.maximum(mb[...], s)
        alpha = jnp.exp(mb[...] - m_new)
        p = jnp.exp(s - m_new)
        lb[...] = alpha * lb[...] + p
        accb[...] = alpha * accb[...] + p * kb
        mb[...] = m_new
        _self_update(qc, nck_ref[...], ncv_ref[...], sc_c, 0.0, mc, lc, accc)
        oa_ref[...] = acca[...] / la[...]
        ol_ref[...] = accb[...] / lb[...]
        oc_ref[...] = accc[...] / lc[...]


def _smain_call(page_table, q_ops, bias, bself, new_rows, caches, layer):
    DB, n_pages = page_table.shape
    group = _pick_tile(n_pages, SUB_PAGES, 1)
    tc = group * PAGE_SIZE
    bias = bias.reshape(DB, 1, n_pages * PAGE_SIZE)
    bself = bself.reshape(DB, 1, LANE)
    per_b = lambda a: pl.BlockSpec((None,) + a.shape[1:], lambda b, c, pt: (b,) + (0,) * (a.ndim - 1))
    in_specs = ([per_b(a) for a in q_ops]
                + [pl.BlockSpec((None, 1, tc), lambda b, c, pt: (b, 0, c)), per_b(bself)]
                + [per_b(a) for a in new_rows])
    page_ops = []
    for cache in caches:
        in_specs += _page_specs(cache, layer, group)
        page_ops += [cache] * group
    out_shapes = [jax.ShapeDtypeStruct((DB, 2 * 8, LANE), F32),
                  jax.ShapeDtypeStruct((DB, 8, B_KV_RANK), F32),
                  jax.ShapeDtypeStruct((DB, 8, LANE), F32)]
    state = []
    for rows, w in ((16, LANE), (8, B_KV_RANK), (8, LANE)):
        state += [pltpu.VMEM((rows, 1), F32), pltpu.VMEM((rows, 1), F32), pltpu.VMEM((rows, w), F32)]
    grid_spec = pltpu.PrefetchScalarGridSpec(
        num_scalar_prefetch=1, grid=(DB, n_pages // group),
        in_specs=in_specs,
        out_specs=[pl.BlockSpec((None,) + s.shape[1:], lambda b, c, pt: (b, 0, 0)) for s in out_shapes],
        scratch_shapes=state)
    return pl.pallas_call(
        functools.partial(_smain_kernel, group=group),
        grid_spec=grid_spec,
        out_shape=out_shapes,
        compiler_params=pltpu.CompilerParams(dimension_semantics=("arbitrary", "arbitrary"),
                                             vmem_limit_bytes=VMEM_LIMIT),
        name="sample_attn",
    )(page_table, *q_ops, bias, bself, *new_rows, *page_ops)


def _scomb_kernel(oa_ref, ol_ref, oc_ref, wv_ref, lq1_ref, lk1_ref, lq2_ref, lk2_ref, gsub_ref, o_ref, *, lam_init):
    pieces = []
    for h in range(A_HEADS):
        g, r = divmod(h, A_HEADS // A_KV_HEADS)
        row = g * 8 + r
        pieces.append(oa_ref[:, row * LANE + g * 64:row * LANE + (g + 1) * 64])
    for h in range(B_HEADS):
        lat = ol_ref[:, h * B_KV_RANK:(h + 1) * B_KV_RANK].astype(BF16)
        pv = _dot(lat, wv_ref[:, _slab(h // 2)])
        pieces.append(pv[:, (h % 2) * 64:(h % 2 + 1) * 64])
    lam = _lambda(lq1_ref, lk1_ref, lq2_ref, lk2_ref, lam_init)
    for hd in range(C_HEADS):
        g = hd // (C_HEADS // C_KV_HEADS)
        o0 = oc_ref[:, (2 * hd) * LANE + g * 64:(2 * hd) * LANE + (g + 1) * 64]
        o1 = oc_ref[:, (2 * hd + 1) * LANE + g * 64:(2 * hd + 1) * LANE + (g + 1) * 64]
        pieces.append(_full_rms(o0 - lam * o1, gsub_ref[...]) * (1.0 - lam_init))
    o_ref[...] = jnp.concatenate(pieces, axis=-1).astype(o_ref.dtype)


def _scomb_call(oa, ol, oc, lw, lam_init):
    DB = oa.shape[0]
    ins = [oa.reshape(DB, -1), ol.reshape(DB, -1), oc.reshape(DB, -1), lw["w_uv"],
           lw["c_lq1"], lw["c_lk1"], lw["c_lq2"], lw["c_lk2"], lw["g_sub"]]
    d_mix = (A_HEADS + B_HEADS + C_HEADS) * 64
    return pl.pallas_call(
        functools.partial(_scomb_kernel, lam_init=lam_init),
        grid=(1,),
        in_specs=[pl.BlockSpec(a.shape, lambda i: (0,) * a.ndim) for a in ins],
        out_specs=pl.BlockSpec((DB, d_mix), lambda i: (0, 0)),
        out_shape=jax.ShapeDtypeStruct((DB, d_mix), BF16),
        name="sample_combine",
    )(*ins)


def _layer_weights(l, P):
    return {k: v[l] for k, v in P.items()}


def kernel(x_prompt, x_sample, cache_a_k, cache_a_v, cache_a_idx_k, cache_b_ckv, cache_b_krope, cache_c_k, cache_c_v, page_table, meta_tokens, g_attn, w_in, a_q_norm, a_k_norm, idx_k_norm, b_q_a_norm, w_uq, b_q_norm, b_kv_a_norm, b_kr_norm, w_uk, w_uv, c_q_norm, c_k_norm, c_lq1, c_lk1, c_lq2, c_lk2, c_sub_norm, w_out, g_ffn, w_gate, w_up, w_down):
    Bp, Sp, D = x_prompt.shape
    DB, S = x_sample.shape[:2]
    depth = w_in.shape[0]
    assert S == 1
    n_pages = page_table.shape[1]
    past_len = n_pages * PAGE_SIZE
    l_p = Sp + N_META
    tq = 128
    l_pad = -(-l_p // tq) * tq
    topk_p = min(TOPK_MAX, Sp // 4)
    topk_s = min(TOPK_MAX, (past_len + S) // 4)

    gA, gC, gT, gB = _gain_rows(a_q_norm, a_k_norm, idx_k_norm, b_kr_norm, b_q_norm, c_q_norm, c_k_norm)
    row = lambda v: v[:, None, :]
    P = {
        "g_attn": row(g_attn), "w_in": _take_cols(w_in, _in_proj_columns(), 0.0).astype(BF16),
        "gA": row(gA), "gC": row(gC), "gT": row(gT), "gB": row(gB),
        "g_cq": row(b_q_a_norm), "g_ckv": row(b_kv_a_norm),
        "w_uq": _take_cols(w_uq, _uq_columns(), 0.0).astype(BF16),
        "w_uk": w_uk.reshape(depth, B_KV_RANK, B_HEADS * B_NOPE).astype(BF16),
        "w_uv": w_uv.reshape(depth, B_KV_RANK, B_HEADS * B_V).astype(BF16),
        "c_lq1": row(c_lq1), "c_lk1": row(c_lk1), "c_lq2": row(c_lq2), "c_lk2": row(c_lk2),
        "g_sub": row(c_sub_norm), "w_out": w_out.astype(BF16), "g_ffn": row(g_ffn),
        "w_gate": w_gate.astype(BF16), "w_up": w_up.astype(BF16), "w_down": w_down.astype(BF16),
    }
    consts = {"m64": _block_ones(64, 1.0 / 64), "m32": _block_ones(32, 1.0 / 32),
              "mT": _tail_norm_matrix(), "ones": jnp.ones((LANE, LANE), BF16)}

    tabs_p = _rope_tables(jnp.arange(l_pad, dtype=I32))
    tabs_s = _rope_tables(jnp.full((S,), past_len, I32))

    meta = jnp.broadcast_to(meta_tokens[None].astype(x_prompt.dtype), (Bp, N_META, D))
    h_p = jnp.concatenate([meta, x_prompt, jnp.zeros((Bp, l_pad - l_p, D), x_prompt.dtype)], axis=1)
    h_s = x_sample.reshape(DB, D)

    names = ["ka", "va", "ik", "ckv", "kr", "kc", "vc"]
    rows_p = {n: [] for n in names}
    rows_s = {n: [] for n in names}
    def feature_major(c):
        perm = (0, 1) + tuple(range(3, c.ndim)) + (2,)
        return c.transpose(perm).reshape(c.shape[:2] + (-1, c.shape[2]))

    cache_idx_t = feature_major(cache_a_idx_k)
    caches = (feature_major(cache_a_k), feature_major(cache_a_v), cache_b_ckv, feature_major(cache_b_krope),
              feature_major(cache_c_k), feature_major(cache_c_v))

    for l in range(depth):
        lw = dict(_layer_weights(l, P), **consts)
        lam_init = 0.8 - 0.6 * math.exp(-0.3 * l)

        pr = _proj_call(h_p, tabs_p, lw, sample=False)
        o = _pattn_call(pr, lw, q0=0, nqb=l_pad // tq, lk=l_pad, tq=tq, topk=topk_p, lam_init=lam_init)
        h_p = _mlp_call(h_p.reshape(Bp * l_pad, D), o.reshape(Bp * l_pad, -1), lw).reshape(Bp, l_pad, D)
        ka_t, va_t, kc_t, vc_t, tail_t = _rows_t_call([pr[n] for n in ("ka", "va", "kc", "vc", "tail")])
        for n, a in (("ka", ka_t), ("va", va_t), ("kc", kc_t), ("vc", vc_t),
                     ("ik", tail_t[:, T_IK:T_IK + IDX_DIM]), ("kr", tail_t[:, T_KR:T_KR + B_ROPE])):
            rows_p[n].append(a[:, :, :l_p])
        rows_p["ckv"].append(pr["ckv"][:, :l_p])

        sr = _proj_call(h_s.reshape(1, DB, D), jnp.broadcast_to(tabs_s, (N_TAB, DB, LANE)), lw, sample=True)
        sr = {k: v[0] for k, v in sr.items()}
        tail = sr["tail"]
        q_idx = sr["qi"].reshape(DB, IDX_HEADS, LANE)[:, :, :IDX_DIM]
        w_idx = tail[:, T_IW:T_IW + IDX_HEADS].reshape(DB, IDX_HEADS, 1)
        s_past = _sidx_call(page_table, q_idx, w_idx, cache_idx_t, l)
        bias, bself = _sselect_call(s_past.reshape(DB, past_len), sr["qi"], tail, topk_s)
        hpg = A_HEADS // A_KV_HEADS
        qa = jnp.pad(sr["qa"].reshape(DB, A_KV_HEADS, hpg, LANE), ((0, 0), (0, 0), (0, 8 - hpg), (0, 0)))
        qabs = jnp.pad(sr["qabs"].reshape(DB, B_HEADS, B_KV_RANK), ((0, 0), (0, 8 - B_HEADS), (0, 0)))
        qr = jnp.pad(sr["qb"].reshape(DB, B_HEADS, 2, LANE)[:, :, 1, T_KR:T_KR + B_ROPE],
                     ((0, 0), (0, 8 - B_HEADS), (0, 0)))
        qc = sr["qc"].reshape(DB, 2 * C_HEADS, LANE)
        kr_new = tail[:, T_KR:T_KR + B_ROPE]
        new_rows = [a.reshape(DB, 1, a.shape[-1]) for a in
                    (sr["ka"], sr["va"], sr["ckv"], kr_new, sr["kc"], sr["vc"])]
        oa, ol, oc = _smain_call(page_table, (qa.reshape(DB, 16, LANE), qabs, qr, qc), bias, bself,
                                 new_rows, caches, l)
        o_s = _scomb_call(oa, ol, oc, lw, lam_init)
        h_s = _mlp_call(h_s, o_s, lw)
        for n in ("ka", "va", "ckv", "kc", "vc"):
            rows_s[n].append(sr[n])
        rows_s["ik"].append(tail[:, T_IK:T_IK + IDX_DIM])
        rows_s["kr"].append(kr_new)

    y_prompt = h_p[:, N_META:l_p]
    y_sample = h_s.reshape(DB, S, D)
    tails = {"ka": (A_KV_HEADS, A_HEAD_DIM), "va": (A_KV_HEADS, A_HEAD_DIM), "ik": (IDX_DIM,),
             "ckv": (B_KV_RANK,), "kr": (B_ROPE,), "kc": (C_KV_HEADS, 2, C_QK), "vc": (C_KV_HEADS, C_V)}
    def prompt_rows(n):
        a = jnp.stack(rows_p[n])
        if n == "ckv":
            return a
        a = a.reshape((depth, Bp) + tails[n] + (l_p,))
        return a.transpose((0, 1, a.ndim - 1) + tuple(range(2, a.ndim - 1)))

    outs_p = [prompt_rows(n) for n in names]
    outs_s = [jnp.stack(rows_s[n]).reshape((depth, DB, S) + tails[n]) for n in names]
    return (y_prompt, y_sample, *outs_p, *outs_s)
```
